```python
import jax, jax.numpy as jnp
from jax import lax
import numpy as np

D_MODEL = 1024
BATCH = 16
SEQ = 2048
DEPTH = 2
DEC_BATCH = 128
DEC_SEQ = 8
PAST_LEN = 8192
PAGE_SIZE = 128

N_MIXERS = 2
N_MLA_LAYERS = (DEPTH + 1) // 2
N_GLA_LAYERS = DEPTH // 2
D_FF = 2816
FFN_RESID = 0.5
EPS = 1e-6

MLA_HEADS = 8
MLA_Q_LORA = 384
MLA_KV_LORA = 256
MLA_NOPE = 128
MLA_ROPE = 64
MLA_V = 128
MLA_SCALE = (MLA_NOPE + MLA_ROPE) ** -0.5
ROPE_THETA = 10000.0
Q_BLOCK = 128

GLA_HEADS = 4
GLA_DK = D_MODEL // 2 // GLA_HEADS
GLA_DV = D_MODEL // GLA_HEADS
GLA_GATE_RANK = 16
GLA_GATE_NORM = 16.0
GLA_CHUNK = 64

kernel_name = 'hybrid_mla_gla_macaron_step'


def rmsnorm(x, g):
    xf = x.astype(jnp.float32)
    y = xf * lax.rsqrt(jnp.mean(xf * xf, axis=-1, keepdims=True) + EPS)
    return (y * g.astype(jnp.float32)).astype(x.dtype)


def ffn_half(h, g_pre, g_post, w_in, w_out):
    a = rmsnorm(h, g_pre) @ w_in
    gate, up = jnp.split(a, 2, axis=-1)
    return h + FFN_RESID * rmsnorm((jax.nn.silu(gate) * up) @ w_out, g_post)


def rope(x, pos):
    half = MLA_ROPE // 2
    inv = ROPE_THETA ** (-jnp.arange(half, dtype=jnp.float32) / half)
    ang = pos.astype(jnp.float32)[:, None] * inv[None, :]
    cos = jnp.cos(ang)[None, :, None, :]
    sin = jnp.sin(ang)[None, :, None, :]
    xf = x.astype(jnp.float32)
    x1, x2 = xf[..., :half], xf[..., half:]
    return jnp.concatenate([x1 * cos - x2 * sin, x2 * cos + x1 * sin], axis=-1).astype(x.dtype)


def mla_project(x, pos, w_down, g_q, w_uq, g_kv):
    B, S, _ = x.shape
    d = x @ w_down
    c_q = rmsnorm(d[..., :MLA_Q_LORA], g_q)
    c_kv = rmsnorm(d[..., MLA_Q_LORA:MLA_Q_LORA + MLA_KV_LORA], g_kv)
    k_pe = rope(d[..., MLA_Q_LORA + MLA_KV_LORA:][:, :, None, :], pos)[:, :, 0, :]
    q = (c_q @ w_uq).reshape(B, S, MLA_HEADS, MLA_NOPE + MLA_ROPE)
    q_nope = q[..., :MLA_NOPE]
    q_pe = rope(q[..., MLA_NOPE:], pos)
    return q_nope, q_pe, c_kv, k_pe


def mla_prompt(x, w_down, g_q, w_uq, g_kv, w_uk, w_uv, w_o):
    B, S, _ = x.shape
    pos = jnp.arange(S)
    q_nope, q_pe, c_kv, k_pe = mla_project(x, pos, w_down, g_q, w_uq, g_kv)
    k_nope = jnp.einsum('bsr,rhd->bshd', c_kv, w_uk)
    v = jnp.einsum('bsr,rhd->bshd', c_kv, w_uv)
    n_blk = S // Q_BLOCK
    qn_b = q_nope.reshape(B, n_blk, Q_BLOCK, MLA_HEADS, MLA_NOPE).swapaxes(0, 1)
    qp_b = q_pe.reshape(B, n_blk, Q_BLOCK, MLA_HEADS, MLA_ROPE).swapaxes(0, 1)
    key_pos = jnp.arange(S)

    def block(args):
        qn, qp, i = args
        s = (jnp.einsum('bqhd,bkhd->bhqk', qn, k_nope)
             + jnp.einsum('bqhr,bkr->bhqk', qp, k_pe)).astype(jnp.float32) * MLA_SCALE
        q_pos = i * Q_BLOCK + jnp.arange(Q_BLOCK)
        s = jnp.where(key_pos[None, :] <= q_pos[:, None], s, -jnp.inf)
        p = jax.nn.softmax(s, axis=-1).astype(v.dtype)
        return jnp.einsum('bhqk,bkhd->bqhd', p, v)

    o = lax.map(block, (qn_b, qp_b, jnp.arange(n_blk)))
    o = o.swapaxes(0, 1).reshape(B, S, MLA_HEADS * MLA_V)
    return o @ w_o, c_kv, k_pe


def mla_sample(x, cache_ckv, cache_kpe, page_table, w_down, g_q, w_uq, g_kv, w_uk, w_uv, w_o):
    B, T, _ = x.shape
    pos = PAST_LEN + jnp.arange(T)
    q_nope, q_pe, c_kv, k_pe = mla_project(x, pos, w_down, g_q, w_uq, g_kv)
    past_ckv = cache_ckv[page_table].reshape(B, -1, MLA_KV_LORA).astype(c_kv.dtype)
    past_kpe = cache_kpe[page_table].reshape(B, -1, MLA_ROPE).astype(k_pe.dtype)
    P = past_ckv.shape[1]
    all_ckv = jnp.concatenate([past_ckv, c_kv], axis=1)
    all_kpe = jnp.concatenate([past_kpe, k_pe], axis=1)
    q_lat = jnp.einsum('bthd,rhd->bthr', q_nope, w_uk)
    s = (jnp.einsum('bthr,bkr->bhtk', q_lat, all_ckv)
         + jnp.einsum('bthd,bkd->bhtk', q_pe, all_kpe)).astype(jnp.float32) * MLA_SCALE
    key_idx = jnp.arange(P + T)
    q_idx = P + jnp.arange(T)
    s = jnp.where(key_idx[None, :] <= q_idx[:, None], s, -jnp.inf)
    p = jax.nn.softmax(s, axis=-1).astype(all_ckv.dtype)
    o_lat = jnp.einsum('bhtk,bkr->bthr', p, all_ckv)
    o = jnp.einsum('bthr,rhd->bthd', o_lat, w_uv).reshape(B, T, MLA_HEADS * MLA_V)
    return o @ w_o, c_kv, k_pe


def gla_project(x, w_in, w_gk1, w_gk2, b_gk):
    B, S, _ = x.shape
    hk, hv = GLA_HEADS * GLA_DK, GLA_HEADS * GLA_DV
    p = x @ w_in
    q, k, v, og = jnp.split(p, [hk, 2 * hk, 2 * hk + hv], axis=-1)
    q = q.reshape(B, S, GLA_HEADS, GLA_DK) * GLA_DK ** -0.5
    k = k.reshape(B, S, GLA_HEADS, GLA_DK)
    v = v.reshape(B, S, GLA_HEADS, GLA_DV)
    gk = jax.nn.log_sigmoid(((x @ w_gk1) @ w_gk2 + b_gk).astype(jnp.float32)) / GLA_GATE_NORM
    gk = gk.reshape(B, S, GLA_HEADS, GLA_DK)
    return q, k, v, gk, og


def gla_chunk(state, q, k, v, g):
    qf, kf, vf = q.astype(jnp.float32), k.astype(jnp.float32), v.astype(jnp.float32)
    G = jnp.cumsum(g, axis=1)
    C = q.shape[1]
    causal = jnp.tril(jnp.ones((C, C), dtype=bool))
    diff = G[:, :, None] - G[:, None, :]
    decay = jnp.exp(jnp.where(causal[None, :, :, None, None], diff, -jnp.inf))
    attn = jnp.sum(qf[:, :, None] * kf[:, None, :] * decay, axis=-1)
    o = (jnp.einsum('bthk,bhkv->bthv', qf * jnp.exp(G), state)
         + jnp.einsum('btsh,bshv->bthv', attn, vf))
    g_last = G[:, -1]
    new_state = (jnp.exp(g_last)[..., None] * state
                 + jnp.einsum('bshk,bshv->bhkv', kf * jnp.exp(g_last[:, None] - G), vf))
    return new_state, o


def gla_output(o, og, g_norm, w_o):
    B, S = o.shape[:2]
    og = og.reshape(B, S, GLA_HEADS, GLA_DV)
    o = rmsnorm(o.astype(og.dtype), g_norm) * jax.nn.silu(og)
    return o.reshape(B, S, GLA_HEADS * GLA_DV) @ w_o


def gla_prompt(x, w_in, w_gk1, w_gk2, b_gk, g_norm, w_o):
    B, S, _ = x.shape
    q, k, v, gk, og = gla_project(x, w_in, w_gk1, w_gk2, b_gk)
    n = S // GLA_CHUNK

    def to_chunks(a):
        return a.reshape(B, n, GLA_CHUNK, *a.shape[2:]).swapaxes(0, 1)

    s0 = jnp.zeros((B, GLA_HEADS, GLA_DK, GLA_DV), jnp.float32)

    def step(state, inp):
        return gla_chunk(state, *inp)

    s_fin, o = lax.scan(step, s0, (to_chunks(q), to_chunks(k), to_chunks(v), to_chunks(gk)))
    o = o.swapaxes(0, 1).reshape(B, S, GLA_HEADS, GLA_DV)
    return gla_output(o, og, g_norm, w_o), s_fin


def gla_sample(x, state, w_in, w_gk1, w_gk2, b_gk, g_norm, w_o):
    q, k, v, gk, og = gla_project(x, w_in, w_gk1, w_gk2, b_gk)
    s_new, o = gla_chunk(state.astype(jnp.float32), q, k, v, gk)
    return gla_output(o, og, g_norm, w_o), s_new


def setup_inputs(seed: int = 0) -> dict:
    key = jax.random.key(seed)
    ks = jax.random.split(key, 24)
    n_pages = PAST_LEN // PAGE_SIZE
    n_pool = (DEC_BATCH * n_pages * 5) // 4

    def w(k, shape, fan_in):
        return jax.random.normal(k, shape, jnp.float32) * fan_in ** -0.5

    def gain(k, shape):
        return 1.0 + 0.02 * jax.random.normal(k, shape, jnp.float32)

    return {
        'x_prompt': jax.random.normal(ks[0], (BATCH, SEQ, D_MODEL), jnp.float32),
        'x_sample': jax.random.normal(ks[1], (DEC_BATCH, DEC_SEQ, D_MODEL), jnp.float32),
        'cache_ckv': jax.random.normal(ks[2], (N_MLA_LAYERS, n_pool, PAGE_SIZE, MLA_KV_LORA), jnp.float32),
        'cache_kpe': jax.random.normal(ks[3], (N_MLA_LAYERS, n_pool, PAGE_SIZE, MLA_ROPE), jnp.float32),
        'state_gla': 0.5 * jax.random.normal(ks[4], (N_GLA_LAYERS, DEC_BATCH, GLA_HEADS, GLA_DK, GLA_DV), jnp.float32),
        'page_table': jax.random.permutation(ks[5], n_pool)[:DEC_BATCH * n_pages].reshape(DEC_BATCH, n_pages).astype(jnp.int32),
        'norm_g': gain(ks[6], (DEPTH, 6, D_MODEL)),
        'ffn_w_in': w(ks[7], (DEPTH, 2, D_MODEL, 2 * D_FF), D_MODEL),
        'ffn_w_out': w(ks[8], (DEPTH, 2, D_FF, D_MODEL), D_FF),
        'mla_w_down': w(ks[9], (N_MLA_LAYERS, D_MODEL, MLA_Q_LORA + MLA_KV_LORA + MLA_ROPE), D_MODEL),
        'mla_g_q': gain(ks[10], (N_MLA_LAYERS, MLA_Q_LORA)),
        'mla_w_uq': w(ks[11], (N_MLA_LAYERS, MLA_Q_LORA, MLA_HEADS * (MLA_NOPE + MLA_ROPE)), MLA_Q_LORA),
        'mla_g_kv': gain(ks[12], (N_MLA_LAYERS, MLA_KV_LORA)),
        'mla_w_uk': w(ks[13], (N_MLA_LAYERS, MLA_KV_LORA, MLA_HEADS, MLA_NOPE), MLA_KV_LORA),
        'mla_w_uv': w(ks[14], (N_MLA_LAYERS, MLA_KV_LORA, MLA_HEADS, MLA_V), MLA_KV_LORA),
        'mla_w_o': w(ks[15], (N_MLA_LAYERS, MLA_HEADS * MLA_V, D_MODEL), MLA_HEADS * MLA_V),
        'gla_w_in': w(ks[16], (N_GLA_LAYERS, D_MODEL, 2 * GLA_HEADS * GLA_DK + 2 * GLA_HEADS * GLA_DV), D_MODEL),
        'gla_w_gk1': w(ks[17], (N_GLA_LAYERS, D_MODEL, GLA_GATE_RANK), D_MODEL),
        'gla_w_gk2': w(ks[18], (N_GLA_LAYERS, GLA_GATE_RANK, GLA_HEADS * GLA_DK), GLA_GATE_RANK),
        'gla_b_gk': 0.1 * jax.random.normal(ks[19], (N_GLA_LAYERS, GLA_HEADS * GLA_DK), jnp.float32),
        'gla_g_norm': gain(ks[20], (N_GLA_LAYERS, GLA_DV)),
        'gla_w_o': w(ks[21], (N_GLA_LAYERS, GLA_HEADS * GLA_DV, D_MODEL), GLA_HEADS * GLA_DV),
    }


def reference(x_prompt, x_sample, cache_ckv, cache_kpe, state_gla, page_table,
              norm_g, ffn_w_in, ffn_w_out,
              mla_w_down, mla_g_q, mla_w_uq, mla_g_kv, mla_w_uk, mla_w_uv, mla_w_o,
              gla_w_in, gla_w_gk1, gla_w_gk2, gla_b_gk, gla_g_norm, gla_w_o):
    yp, ys = x_prompt, x_sample
    ckv_p, kpe_p, ckv_s, kpe_s, st_p, st_s = [], [], [], [], [], []
    for i in range(DEPTH):
        g = norm_g[i]
        yp = ffn_half(yp, g[0], g[1], ffn_w_in[i, 0], ffn_w_out[i, 0])
        ys = ffn_half(ys, g[0], g[1], ffn_w_in[i, 0], ffn_w_out[i, 0])
        j = i // N_MIXERS
        if i % N_MIXERS == 0:
            mla_w = (mla_w_down[j], mla_g_q[j], mla_w_uq[j], mla_g_kv[j], mla_w_uk[j], mla_w_uv[j], mla_w_o[j])
            mp, c_p, k_p = mla_prompt(rmsnorm(yp, g[2]), *mla_w)
            ms, c_s, k_s = mla_sample(rmsnorm(ys, g[2]), cache_ckv[j], cache_kpe[j], page_table, *mla_w)
            ckv_p.append(c_p)
            kpe_p.append(k_p)
            ckv_s.append(c_s)
            kpe_s.append(k_s)
        else:
            gla_w = (gla_w_in[j], gla_w_gk1[j], gla_w_gk2[j], gla_b_gk[j], gla_g_norm[j], gla_w_o[j])
            mp, s_p = gla_prompt(rmsnorm(yp, g[2]), *gla_w)
            ms, s_s = gla_sample(rmsnorm(ys, g[2]), state_gla[j], *gla_w)
            st_p.append(s_p)
            st_s.append(s_s)
        yp = yp + rmsnorm(mp, g[3])
        ys = ys + rmsnorm(ms, g[3])
        yp = ffn_half(yp, g[4], g[5], ffn_w_in[i, 1], ffn_w_out[i, 1])
        ys = ffn_half(ys, g[4], g[5], ffn_w_in[i, 1], ffn_w_out[i, 1])
    new_ckv_prompt = jnp.stack(ckv_p, axis=0)
    new_kpe_prompt = jnp.stack(kpe_p, axis=0)
    new_ckv_sample = jnp.stack(ckv_s, axis=0)
    new_kpe_sample = jnp.stack(kpe_s, axis=0)
    new_state_prompt = jnp.stack(st_p, axis=0)
    new_state_sample = jnp.stack(st_s, axis=0)
    return (yp, ys, new_ckv_prompt, new_kpe_prompt, new_ckv_sample, new_kpe_sample, new_state_prompt, new_state_sample)
```

```python
import functools

import jax
import jax.numpy as jnp
from jax import lax
from jax.experimental import pallas as pl
from jax.experimental.pallas import tpu as pltpu

F32 = jnp.float32
BF16 = jnp.bfloat16

EPS = 1e-6
FFN_RESID = 0.5

MLA_HEADS = 8
MLA_Q_LORA = 384
MLA_KV_LORA = 256
MLA_NOPE = 128
MLA_ROPE = 64
MLA_V = 128
MLA_SCALE = (MLA_NOPE + MLA_ROPE) ** -0.5
ROPE_THETA = 10000.0

GLA_HEADS = 4
GLA_GATE_NORM = 16.0

NEG_BIG = -1e30
SUBLANES = 8
V7X_VMEM_LIMIT_BYTES = 56 * 1024 * 1024

FFN_ROWS = 512
PROJ_ROWS = 512
OUT_ROWS = 1024
ATTN_BLOCK = 512
GLA_CHUNK = 128
DECODE_PAGES = 32


def _params(n_axes):
    return pltpu.CompilerParams(
        dimension_semantics=("arbitrary",) * n_axes,
        vmem_limit_bytes=V7X_VMEM_LIMIT_BYTES,
    )


def _dot(a, b):
    return jnp.dot(a, b, preferred_element_type=F32)


def _dot_nt(a, b):
    return lax.dot_general(a, b, (((1,), (1,)), ((), ())), preferred_element_type=F32)


def _rms(x, g):
    ms = jnp.mean(x * x, axis=-1, keepdims=True)
    return x * lax.rsqrt(ms + EPS) * g


def _const_spec(shape, single_buffer=False):
    zeros = (0,) * len(shape)
    if single_buffer:
        return pl.BlockSpec(shape, lambda *_: zeros, pipeline_mode=pl.Buffered(1))
    return pl.BlockSpec(shape, lambda *_: zeros)


def _ffn_kernel(x_ref, gpre_ref, gpost_ref, win_ref, wout_ref, o_ref):
    x = x_ref[...]
    h = _rms(x, gpre_ref[...]).astype(BF16)
    a = _dot(h, win_ref[...])
    f = wout_ref.shape[0]
    gate = a[:, :f]
    up = a[:, f:]
    act = (gate * jax.nn.sigmoid(gate) * up).astype(BF16)
    o = _dot(act, wout_ref[...])
    o_ref[...] = x + FFN_RESID * _rms(o, gpost_ref[...])


def _ffn_half(x, g_pre, g_post, w_in, w_out):
    n, d = x.shape
    f = w_out.shape[0]
    tm = min(FFN_ROWS, n)
    row = lambda i: (i, 0)
    return pl.pallas_call(
        _ffn_kernel,
        out_shape=jax.ShapeDtypeStruct((n, d), F32),
        grid=(n // tm,),
        in_specs=[
            pl.BlockSpec((tm, d), row),
            _const_spec((1, d)),
            _const_spec((1, d)),
            _const_spec((d, 2 * f), single_buffer=True),
            _const_spec((f, d), single_buffer=True),
        ],
        out_specs=pl.BlockSpec((tm, d), row),
        compiler_params=_params(1),
        name="ffn_half",
    )(x, g_pre, g_post, w_in, w_out)


def _out_proj_kernel(o_ref, w_ref, y_ref, g_ref, out_ref):
    out_ref[...] = y_ref[...] + _rms(_dot(o_ref[...], w_ref[...]), g_ref[...])


def _out_proj(o, w_o, y, g):
    n, d = y.shape
    k = o.shape[1]
    tm = min(OUT_ROWS, n)
    row = lambda i: (i, 0)
    return pl.pallas_call(
        _out_proj_kernel,
        out_shape=jax.ShapeDtypeStruct((n, d), F32),
        grid=(n // tm,),
        in_specs=[
            pl.BlockSpec((tm, k), row),
            _const_spec((k, d)),
            pl.BlockSpec((tm, d), row),
            _const_spec((1, d)),
        ],
        out_specs=pl.BlockSpec((tm, d), row),
        compiler_params=_params(1),
        name="out_proj",
    )(o, w_o, y, g)


def _mla_proj_kernel(x_ref, gm_ref, wd_ref, gq_ref, gkv_ref, wuq_ref, cos_ref, sin_ref,
                     wkv_ref, ckv_ref, kpe_ref, *q_refs, absorbed):
    h, nope, rope, kvl = MLA_HEADS, MLA_NOPE, MLA_ROPE, MLA_KV_LORA
    xn = _rms(x_ref[...], gm_ref[...]).astype(BF16)
    d = _dot(xn, wd_ref[...])
    cq = _rms(d[:, :MLA_Q_LORA], gq_ref[...])
    ckv = _rms(d[:, MLA_Q_LORA:MLA_Q_LORA + kvl], gkv_ref[...])
    cos = cos_ref[...]
    sin = sin_ref[...]
    off = MLA_Q_LORA + kvl
    ckv_ref[...] = ckv
    kpe_ref[...] = d[:, off:off + rope] * cos[:, :rope] + d[:, off + rope:] * sin[:, :rope]

    q = _dot(cq.astype(BF16), wuq_ref[...])
    hn, hr = h * nope, h * rope
    qn = (q[:, :hn] * MLA_SCALE).astype(BF16)
    qp = ((q[:, hn:hn + hr] * cos + q[:, hn + hr:] * sin) * MLA_SCALE).astype(BF16)
    if absorbed:
        qlat_ref, qp_ref = q_refs
        for i in range(h):
            qlat_ref[:, i * kvl:(i + 1) * kvl] = _dot(
                qn[:, i * nope:(i + 1) * nope], wkv_ref[i]).astype(BF16)
        qp_ref[...] = qp
    else:
        qn_ref, qp_ref, kn_ref, v_ref = q_refs
        qn_ref[...] = qn
        for i in range(h):
            qp_ref[i] = qp[:, i * rope:(i + 1) * rope]
        kv = _dot(ckv.astype(BF16), wkv_ref[...])
        kn_ref[...] = kv[:, :hn].astype(BF16)
        v_ref[...] = kv[:, hn:].astype(BF16)


def _mla_proj(x, g_mix, w_down, g_q, w_uq, g_kv, cos, sin, w_kv, *, absorbed):
    n, d = x.shape
    h, nope, rope, kvl = MLA_HEADS, MLA_NOPE, MLA_ROPE, MLA_KV_LORA
    tm = min(PROJ_ROWS, n)
    n_pos = cos.shape[0] // tm
    row = lambda i: (i, 0)
    pos = lambda i: (i % n_pos, 0)
    out_shape = [jax.ShapeDtypeStruct((n, kvl), F32), jax.ShapeDtypeStruct((n, rope), F32)]
    out_specs = [pl.BlockSpec((tm, kvl), row), pl.BlockSpec((tm, rope), row)]
    if absorbed:
        out_shape += [jax.ShapeDtypeStruct((n, h * kvl), BF16),
                      jax.ShapeDtypeStruct((n, h * rope), BF16)]
        out_specs += [pl.BlockSpec((tm, h * kvl), row), pl.BlockSpec((tm, h * rope), row)]
    else:
        out_shape += [jax.ShapeDtypeStruct((n, h * nope), BF16),
                      jax.ShapeDtypeStruct((h, n, rope), BF16),
                      jax.ShapeDtypeStruct((n, h * nope), BF16),
                      jax.ShapeDtypeStruct((n, h * MLA_V), BF16)]
        out_specs += [pl.BlockSpec((tm, h * nope), row),
                      pl.BlockSpec((h, tm, rope), lambda i: (0, i, 0)),
                      pl.BlockSpec((tm, h * nope), row),
                      pl.BlockSpec((tm, h * MLA_V), row)]
    return pl.pallas_call(
        functools.partial(_mla_proj_kernel, absorbed=absorbed),
        out_shape=out_shape,
        grid=(n // tm,),
        in_specs=[
            pl.BlockSpec((tm, d), row),
            _const_spec((1, d)),
            _const_spec(w_down.shape),
            _const_spec((1, MLA_Q_LORA)),
            _const_spec((1, kvl)),
            _const_spec(w_uq.shape),
            pl.BlockSpec((tm, h * rope), pos),
            pl.BlockSpec((tm, h * rope), pos),
            _const_spec(w_kv.shape),
        ],
        out_specs=out_specs,
        compiler_params=_params(1),
        name="mla_proj_absorbed" if absorbed else "mla_proj",
    )(x, g_mix, w_down, g_q, g_kv, w_uq, cos, sin, w_kv)


def _softmax_update(s, v, m_prev, l_prev, acc_prev):
    m_new = jnp.maximum(m_prev, jnp.max(s, axis=-1, keepdims=True))
    alpha = jnp.exp(m_prev - m_new)
    p = jnp.exp(s - m_new)
    l_new = alpha * l_prev + jnp.sum(p, axis=-1, keepdims=True)
    acc_new = alpha * acc_prev + _dot(p.astype(BF16), v)
    return m_new, l_new, acc_new


def _mla_attn_kernel(qn_ref, qp_ref, kn_ref, kpe_ref, v_ref, o_ref):
    tq = qn_ref.shape[0]
    qi = pl.program_id(2)
    qn = qn_ref[...]
    qp = qp_ref[0]

    def block(j, carry, diagonal):
        ks = pl.ds(pl.multiple_of(j * tq, tq), tq)
        s = _dot_nt(qn, kn_ref[ks, :]) + _dot_nt(qp, kpe_ref[ks, :].astype(BF16))
        if diagonal:
            row = lax.broadcasted_iota(jnp.int32, s.shape, 0)
            col = lax.broadcasted_iota(jnp.int32, s.shape, 1)
            s = jnp.where(col <= row, s, NEG_BIG)
        return _softmax_update(s, v_ref[ks, :], *carry)

    init = (jnp.full((tq, 1), NEG_BIG, F32), jnp.zeros((tq, 1), F32),
            jnp.zeros((tq, v_ref.shape[1]), F32))
    carry = lax.fori_loop(0, qi, lambda j, c: block(j, c, False), init)
    _, l, acc = block(qi, carry, True)
    o_ref[...] = (acc / l).astype(o_ref.dtype)


def _mla_attention(qn, qp, kn, kpe, v, batch, seq):
    n = qn.shape[0]
    tq = min(ATTN_BLOCK, seq)
    nq = seq // tq
    qblk = lambda b, h, i: (b * nq + i, h)
    kblk = lambda b, h, i: (b, h)
    return pl.pallas_call(
        _mla_attn_kernel,
        out_shape=jax.ShapeDtypeStruct((n, MLA_HEADS * MLA_V), BF16),
        grid=(batch, MLA_HEADS, nq),
        in_specs=[
            pl.BlockSpec((tq, MLA_NOPE), qblk),
            pl.BlockSpec((1, tq, MLA_ROPE), lambda b, h, i: (h, b * nq + i, 0)),
            pl.BlockSpec((seq, MLA_NOPE), kblk),
            pl.BlockSpec((seq, MLA_ROPE), lambda b, h, i: (b, 0)),
            pl.BlockSpec((seq, MLA_V), kblk),
        ],
        out_specs=pl.BlockSpec((tq, MLA_V), qblk),
        compiler_params=_params(3),
        name="mla_attention",
    )(qn, qp, kn, kpe, v)


def _mla_decode_kernel(pt_ref, qlat_ref, qpe_ref, nckv_ref, nkpe_ref, cckv_hbm, ckpe_hbm,
                       o_ref, ckv_buf, kpe_buf, sem, m_sc, l_sc, acc_sc, *, pages, page_rows):
    c = pl.program_id(1)
    n_chunks = pl.num_programs(1)
    step = pl.program_id(0) * n_chunks + c
    n_steps = pl.num_programs(0) * n_chunks
    slot = lax.rem(step, 2)

    def page_copies(step_, slot_):
        copies = []
        for i in range(pages):
            page = pt_ref[step_ * pages + i]
            rows = pl.ds(i * page_rows, page_rows)
            copies.append(pltpu.make_async_copy(
                cckv_hbm.at[page], ckv_buf.at[slot_, rows, :], sem.at[0, slot_]))
            copies.append(pltpu.make_async_copy(
                ckpe_hbm.at[page], kpe_buf.at[slot_, rows, :], sem.at[1, slot_]))
        return copies

    @pl.when(step == 0)
    def _():
        for cp in page_copies(step, slot):
            cp.start()

    @pl.when(step + 1 < n_steps)
    def _():
        for cp in page_copies(step + 1, 1 - slot):
            cp.start()

    @pl.when(c == 0)
    def _():
        m_sc[...] = jnp.full(m_sc.shape, NEG_BIG, F32)
        l_sc[...] = jnp.zeros(l_sc.shape, F32)
        acc_sc[...] = jnp.zeros(acc_sc.shape, F32)

    ql = qlat_ref[0]
    qp = qpe_ref[0]

    def attend(keys, keys_pe, mask=None):
        s = _dot_nt(ql, keys) + _dot_nt(qp, keys_pe)
        if mask is not None:
            s = jnp.where(mask, s, NEG_BIG)
        m, l, acc = _softmax_update(s, keys, m_sc[...], l_sc[...], acc_sc[...])
        m_sc[...] = m
        l_sc[...] = l
        acc_sc[...] = acc

    for cp in page_copies(step, slot):
        cp.wait()
    attend(ckv_buf[slot].astype(BF16), kpe_buf[slot].astype(BF16))

    @pl.when(c == n_chunks - 1)
    def _():
        t_new = nckv_ref.shape[1]
        pad = page_rows - t_new
        nk = jnp.concatenate([nckv_ref[0], jnp.zeros((pad, nckv_ref.shape[2]), F32)], axis=0)
        npe = jnp.concatenate([nkpe_ref[0], jnp.zeros((pad, nkpe_ref.shape[2]), F32)], axis=0)
        shape = (ql.shape[0], page_rows)
        q_tok = lax.broadcasted_iota(jnp.int32, shape, 0) // MLA_HEADS
        k_tok = lax.broadcasted_iota(jnp.int32, shape, 1)
        attend(nk.astype(BF16), npe.astype(BF16), mask=k_tok <= q_tok)
        o_ref[0] = acc_sc[...] / l_sc[...]


def _mla_decode(page_table, qlat, qpe, new_ckv, new_kpe, cache_ckv, cache_kpe):
    bs, rows, kvl = qlat.shape
    n_pages = page_table.shape[1]
    page_rows = cache_ckv.shape[1]
    pages = min(DECODE_PAGES, n_pages)
    t_new = new_ckv.shape[1]
    per_b = lambda b, c, pt: (b, 0, 0)
    grid_spec = pltpu.PrefetchScalarGridSpec(
        num_scalar_prefetch=1,
        grid=(bs, n_pages // pages),
        in_specs=[
            pl.BlockSpec((1, rows, kvl), per_b),
            pl.BlockSpec((1, rows, MLA_ROPE), per_b),
            pl.BlockSpec((1, t_new, kvl), per_b),
            pl.BlockSpec((1, t_new, MLA_ROPE), per_b),
            pl.BlockSpec(memory_space=pl.ANY),
            pl.BlockSpec(memory_space=pl.ANY),
        ],
        out_specs=pl.BlockSpec((1, rows, kvl), per_b),
        scratch_shapes=[
            pltpu.VMEM((2, pages * page_rows, kvl), F32),
            pltpu.VMEM((2, pages * page_rows, MLA_ROPE), F32),
            pltpu.SemaphoreType.DMA((2, 2)),
            pltpu.VMEM((rows, 1), F32),
            pltpu.VMEM((rows, 1), F32),
            pltpu.VMEM((rows, kvl), F32),
        ],
    )
    return pl.pallas_call(
        functools.partial(_mla_decode_kernel, pages=pages, page_rows=page_rows),
        out_shape=jax.ShapeDtypeStruct((bs, rows, kvl), F32),
        grid_spec=grid_spec,
        compiler_params=_params(2),
        name="mla_decode",
    )(page_table.reshape(-1), qlat, qpe, new_ckv, new_kpe, cache_ckv, cache_kpe)


def _mla_decode_out_kernel(olat_ref, wuv_ref, wo_ref, y_ref, g_ref, out_ref):
    kvl = MLA_KV_LORA
    olat = olat_ref[...].astype(BF16)
    o = jnp.concatenate(
        [_dot(olat[:, i * kvl:(i + 1) * kvl], wuv_ref[i]) for i in range(MLA_HEADS)], axis=1)
    out_ref[...] = y_ref[...] + _rms(_dot(o.astype(BF16), wo_ref[...]), g_ref[...])


def _mla_decode_out(olat, w_uv, w_o, y, g):
    n, d = y.shape
    return pl.pallas_call(
        _mla_decode_out_kernel,
        out_shape=jax.ShapeDtypeStruct((n, d), F32),
        grid=(1,),
        in_specs=[_const_spec(olat.shape), _const_spec(w_uv.shape), _const_spec(w_o.shape),
                  _const_spec((n, d)), _const_spec((1, d))],
        out_specs=_const_spec((n, d)),
        compiler_params=_params(1),
        name="mla_decode_out",
    )(olat, w_uv, w_o, y, g)


def _group_cumsum(x, group):
    pos = lax.broadcasted_iota(jnp.int32, x.shape, 0) & (group - 1)
    shift = 1
    while shift < group:
        x = x + jnp.where(pos >= shift, pltpu.roll(x, shift, 0), 0.0)
        shift *= 2
    return x


def _gla_proj_kernel(x_ref, gm_ref, win_ref, wg1_ref, wg2_ref, bg_ref,
                     q_ref, k_ref, v_ref, og_ref, gcum_ref, *, chunk):
    hk = q_ref.shape[1]
    hv = v_ref.shape[1]
    dk = hk // GLA_HEADS
    xn = _rms(x_ref[...], gm_ref[...]).astype(BF16)
    p = _dot(xn, win_ref[...])
    q_ref[...] = p[:, :hk] * dk ** -0.5
    k_ref[...] = p[:, hk:2 * hk]
    v_ref[...] = p[:, 2 * hk:2 * hk + hv].astype(BF16)
    og_ref[...] = p[:, 2 * hk + hv:]
    z = _dot(_dot(xn, wg1_ref[...]).astype(BF16), wg2_ref[...]) + bg_ref[...]
    g = -(jnp.maximum(-z, 0.0) + jnp.log1p(jnp.exp(-jnp.abs(z)))) / GLA_GATE_NORM
    gcum_ref[...] = _group_cumsum(g, chunk)


def _gla_proj(x, g_mix, w_in, w_gk1, w_gk2, b_gk, chunk):
    n, d = x.shape
    hk = w_gk2.shape[1]
    hv = (w_in.shape[1] - 2 * hk) // 2
    tm = min(PROJ_ROWS, n)
    row = lambda i: (i, 0)
    return pl.pallas_call(
        functools.partial(_gla_proj_kernel, chunk=chunk),
        out_shape=[jax.ShapeDtypeStruct((n, hk), F32), jax.ShapeDtypeStruct((n, hk), F32),
                   jax.ShapeDtypeStruct((n, hv), BF16), jax.ShapeDtypeStruct((n, hv), F32),
                   jax.ShapeDtypeStruct((n, hk), F32)],
        grid=(n // tm,),
        in_specs=[pl.BlockSpec((tm, d), row), _const_spec((1, d)), _const_spec(w_in.shape),
                  _const_spec(w_gk1.shape), _const_spec(w_gk2.shape), _const_spec((1, hk))],
        out_specs=[pl.BlockSpec((tm, hk), row), pl.BlockSpec((tm, hk), row),
                   pl.BlockSpec((tm, hv), row), pl.BlockSpec((tm, hv), row),
                   pl.BlockSpec((tm, hk), row)],
        compiler_params=_params(1),
        name="gla_proj",
    )(x, g_mix, w_in, w_gk1, w_gk2, b_gk)


def _bcast_group_row(x, group, idx):
    rows, lanes = x.shape
    if group == rows:
        return jnp.broadcast_to(x[idx:idx + 1, :], (rows, lanes))
    x3 = x.reshape(rows // group, group, lanes)
    return jnp.broadcast_to(x3[:, idx:idx + 1, :], x3.shape).reshape(rows, lanes)


def _pad_rows(x, rows):
    if x.shape[0] == rows:
        return x
    return jnp.concatenate([x, jnp.zeros((rows - x.shape[0], x.shape[1]), x.dtype)], axis=0)


def _gla_head(q, k, gcum, v, state):
    c, dk = q.shape
    dv = v.shape[1]
    mxu_rows = max(c, 128)
    row = lax.broadcasted_iota(jnp.int32, (c, c), 0)
    col = lax.broadcasted_iota(jnp.int32, (c, c), 1)

    cols = []
    for s in range(SUBLANES):
        ks = _bcast_group_row(k, SUBLANES, s)
        gs = _bcast_group_row(gcum, SUBLANES, s)
        e = jnp.exp(jnp.minimum(gcum - gs, 0.0))
        cols.append(jnp.sum(q * ks * e, axis=-1, keepdims=True))
    if c == SUBLANES:
        sub = lax.broadcasted_iota(jnp.int32, (c, 1), 0)
        vf = v.astype(F32)
        o = jnp.zeros((c, dv), F32)
        for s in range(SUBLANES):
            o = o + jnp.where(sub >= s, cols[s], 0.0) * vf[s:s + 1, :]
    else:
        attn = jnp.zeros((c, c), F32)
        tile0 = row & ~(SUBLANES - 1)
        sub = row & (SUBLANES - 1)
        for s in range(SUBLANES):
            attn = attn + jnp.where((col == tile0 + s) & (sub >= s), cols[s], 0.0)
        half = SUBLANES
        while half < c:
            gref = _bcast_group_row(gcum, 2 * half, half - 1)
            e = jnp.exp(-jnp.abs(gcum - gref))
            second = (lax.broadcasted_iota(jnp.int32, (c, dk), 0) & half) != 0
            qe = jnp.where(second, q * e, 0.0).astype(BF16)
            ke = jnp.where(second, 0.0, k * e).astype(BF16)
            attn = attn + jnp.where((row ^ col) < 2 * half, _dot_nt(qe, ke), 0.0)
            half *= 2
        o = _dot(attn.astype(BF16), v)

    qg = _pad_rows(q * jnp.exp(gcum), max(c, 2 * SUBLANES)).astype(BF16)
    o = o + _dot(qg, state.astype(BF16))[:c]
    g_last = gcum[c - 1:c, :]
    kg = _pad_rows(k * jnp.exp(g_last - gcum), mxu_rows)
    v_rows = v if c == mxu_rows else _pad_rows(v.astype(F32), mxu_rows).astype(BF16)
    upd = _dot(kg.T.astype(BF16), v_rows)
    decay = jnp.broadcast_to(jnp.exp(g_last), (dk, dk)).T
    decay = jnp.concatenate([decay] * (dv // dk), axis=1)
    return o, decay * state + upd


def _gla_chunk_kernel(q_ref, k_ref, g_ref, v_ref, og_ref, s0_ref, gn_ref, o_ref, s_ref, state):
    c = pl.program_id(1)
    dk = q_ref.shape[1] // GLA_HEADS
    dv = v_ref.shape[1] // GLA_HEADS

    @pl.when(c == 0)
    def _():
        state[...] = s0_ref[0]

    for h in range(GLA_HEADS):
        ksl = slice(h * dk, (h + 1) * dk)
        vsl = slice(h * dv, (h + 1) * dv)
        o, new_state = _gla_head(q_ref[:, ksl], k_ref[:, ksl], g_ref[:, ksl], v_ref[:, vsl],
                                 state[h])
        state[h] = new_state
        og = og_ref[:, vsl]
        o_ref[:, vsl] = (_rms(o, gn_ref[...]) * (og * jax.nn.sigmoid(og))).astype(o_ref.dtype)

    @pl.when(c == pl.num_programs(1) - 1)
    def _():
        s_ref[0] = state[...]


def _gla_chunks(q, k, gcum, v, og, s0, g_norm, chunk):
    n, hk = q.shape
    hv = v.shape[1]
    batch = s0.shape[0]
    n_chunks = n // batch // chunk
    blk = lambda b, c: (b * n_chunks + c, 0)
    st = lambda b, c: (b, 0, 0, 0)
    return pl.pallas_call(
        _gla_chunk_kernel,
        out_shape=[jax.ShapeDtypeStruct((n, hv), BF16), jax.ShapeDtypeStruct(s0.shape, F32)],
        grid=(batch, n_chunks),
        in_specs=[pl.BlockSpec((chunk, hk), blk), pl.BlockSpec((chunk, hk), blk),
                  pl.BlockSpec((chunk, hk), blk), pl.BlockSpec((chunk, hv), blk),
                  pl.BlockSpec((chunk, hv), blk), pl.BlockSpec((1,) + s0.shape[1:], st),
                  _const_spec(g_norm.shape)],
        out_specs=[pl.BlockSpec((chunk, hv), blk), pl.BlockSpec((1,) + s0.shape[1:], st)],
        scratch_shapes=[pltpu.VMEM(s0.shape[1:], F32)],
        compiler_params=_params(2),
        name="gla_chunks",
    )(q, k, gcum, v, og, s0, g_norm)


def _rope_tables(pos, heads):
    half = MLA_ROPE // 2
    inv = ROPE_THETA ** (-jnp.arange(half, dtype=F32) / half)
    ang = pos.astype(F32)[:, None] * inv[None, :]
    cos, sin = jnp.cos(ang), jnp.sin(ang)
    cos2 = jnp.concatenate([cos, cos], axis=-1)
    sin2 = jnp.concatenate([-sin, sin], axis=-1)
    return jnp.tile(cos2, (1, heads)), jnp.tile(sin2, (1, heads))


def _swap_halves(w):
    half = w.shape[-1] // 2
    return jnp.concatenate([w[..., half:], w[..., :half]], axis=-1)


def _mla_weights(w_down, w_uq, w_uk, w_uv, w_o):
    h, nope, rope, kvl = MLA_HEADS, MLA_NOPE, MLA_ROPE, MLA_KV_LORA
    off = MLA_Q_LORA + kvl
    wd = jnp.concatenate([w_down, _swap_halves(w_down[:, off:])], axis=1).astype(BF16)
    wq = w_uq.reshape(MLA_Q_LORA, h, nope + rope)
    wq_pe = wq[:, :, nope:]
    wuq = jnp.concatenate([
        wq[:, :, :nope].reshape(MLA_Q_LORA, h * nope),
        wq_pe.reshape(MLA_Q_LORA, h * rope),
        _swap_halves(wq_pe).reshape(MLA_Q_LORA, h * rope)], axis=1).astype(BF16)
    wkv = jnp.concatenate([w_uk.reshape(kvl, h * nope), w_uv.reshape(kvl, h * MLA_V)],
                          axis=1).astype(BF16)
    wuk_t = w_uk.transpose(1, 2, 0).astype(BF16)
    wuv_h = w_uv.transpose(1, 0, 2).astype(BF16)
    return wd, wuq, wkv, wuk_t, wuv_h, w_o.astype(BF16)


def _mla_layer(yp, ys, g_mix, g_post, batch, seq, dec_batch, dec_seq,
               cache_ckv, cache_kpe, page_table, w_down, g_q, w_uq, g_kv, w_uk, w_uv, w_o):
    h = MLA_HEADS
    wd, wuq, wkv, wuk_t, wuv_h, wo = _mla_weights(w_down, w_uq, w_uk, w_uv, w_o)
    g_q = g_q[None, :]
    g_kv = g_kv[None, :]

    cos_p, sin_p = _rope_tables(jnp.arange(seq), h)
    ckv_p, kpe_p, qn, qp, kn, v = _mla_proj(yp, g_mix, wd, g_q, wuq, g_kv, cos_p, sin_p, wkv,
                                            absorbed=False)
    o_p = _mla_attention(qn, qp, kn, kpe_p, v, batch, seq)
    yp = _out_proj(o_p, wo, yp, g_post)

    past = page_table.shape[1] * cache_ckv.shape[1]
    cos_s, sin_s = _rope_tables(past + jnp.arange(dec_seq), h)
    cos_s = jnp.tile(cos_s, (dec_batch, 1))
    sin_s = jnp.tile(sin_s, (dec_batch, 1))
    ckv_s, kpe_s, qlat, qpe = _mla_proj(ys, g_mix, wd, g_q, wuq, g_kv, cos_s, sin_s, wuk_t,
                                        absorbed=True)
    rows = dec_seq * h
    olat = _mla_decode(page_table,
                       qlat.reshape(dec_batch, rows, MLA_KV_LORA),
                       qpe.reshape(dec_batch, rows, MLA_ROPE),
                       ckv_s.reshape(dec_batch, dec_seq, MLA_KV_LORA),
                       kpe_s.reshape(dec_batch, dec_seq, MLA_ROPE),
                       cache_ckv, cache_kpe)
    ys = _mla_decode_out(olat.reshape(dec_batch * dec_seq, h * MLA_KV_LORA), wuv_h, wo, ys, g_post)
    return yp, ys, ckv_p, kpe_p, ckv_s, kpe_s


def _gla_layer(yp, ys, g_mix, g_post, batch, seq, dec_seq, state,
               w_in, w_gk1, w_gk2, b_gk, g_norm, w_o):
    w_in = w_in.astype(BF16)
    w_gk1 = w_gk1.astype(BF16)
    w_gk2 = w_gk2.astype(BF16)
    w_o = w_o.astype(BF16)
    b_gk = b_gk[None, :]
    g_norm = g_norm[None, :]
    outs = []
    for y, s0, chunk in ((yp, jnp.zeros((batch,) + state.shape[1:], F32), min(GLA_CHUNK, seq)),
                         (ys, state, dec_seq)):
        q, k, v, og, gcum = _gla_proj(y, g_mix, w_in, w_gk1, w_gk2, b_gk, chunk)
        o, s_new = _gla_chunks(q, k, gcum, v, og, s0, g_norm, chunk)
        outs += [_out_proj(o, w_o, y, g_post), s_new]
    return outs


def kernel(x_prompt, x_sample, cache_ckv, cache_kpe, state_gla, page_table, norm_g, ffn_w_in,
           ffn_w_out, mla_w_down, mla_g_q, mla_w_uq, mla_g_kv, mla_w_uk, mla_w_uv, mla_w_o,
           gla_w_in, gla_w_gk1, gla_w_gk2, gla_b_gk, gla_g_norm, gla_w_o):
    batch, seq, d = x_prompt.shape
    dec_batch, dec_seq, _ = x_sample.shape
    yp = x_prompt.reshape(batch * seq, d)
    ys = x_sample.reshape(dec_batch * dec_seq, d)
    w_in = ffn_w_in.astype(BF16)
    w_out = ffn_w_out.astype(BF16)
    ckv_p, kpe_p, ckv_s, kpe_s, st_p, st_s = [], [], [], [], [], []
    for i in range(norm_g.shape[0]):
        g = [norm_g[i, k][None, :] for k in range(norm_g.shape[1])]
        yp = _ffn_half(yp, g[0], g[1], w_in[i, 0], w_out[i, 0])
        ys = _ffn_half(ys, g[0], g[1], w_in[i, 0], w_out[i, 0])
        j = i // 2
        if i % 2 == 0:
            yp, ys, c_p, k_p, c_s, k_s = _mla_layer(
                yp, ys, g[2], g[3], batch, seq, dec_batch, dec_seq,
                cache_ckv[j], cache_kpe[j], page_table,
                mla_w_down[j], mla_g_q[j], mla_w_uq[j], mla_g_kv[j], mla_w_uk[j], mla_w_uv[j],
                mla_w_o[j])
            ckv_p.append(c_p.reshape(batch, seq, -1))
            kpe_p.append(k_p.reshape(batch, seq, -1))
            ckv_s.append(c_s.reshape(dec_batch, dec_seq, -1))
            kpe_s.append(k_s.reshape(dec_batch, dec_seq, -1))
        else:
            yp, s_p, ys, s_s = _gla_layer(
                yp, ys, g[2], g[3], batch, seq, dec_seq, state_gla[j],
                gla_w_in[j], gla_w_gk1[j], gla_w_gk2[j], gla_b_gk[j], gla_g_norm[j], gla_w_o[j])
            st_p.append(s_p)
            st_s.append(s_s)
        yp = _ffn_half(yp, g[4], g[5], w_in[i, 1], w_out[i, 1])
        ys = _ffn_half(ys, g[4], g[5], w_in[i, 1], w_out[i, 1])
    return (yp.reshape(batch, seq, d), ys.reshape(dec_batch, dec_seq, d),
            jnp.stack(ckv_p), jnp.stack(kpe_p), jnp.stack(ckv_s), jnp.stack(kpe_s),
            jnp.stack(st_p), jnp.stack(st_s))
```
